```python
import jax, jax.numpy as jnp
from jax import lax
import numpy as np

D_MODEL = 4096
BATCH = 8
SEQ = 2048
DEPTH = 2

N_A_LAYERS = DEPTH // 2
N_B_LAYERS = DEPTH - N_A_LAYERS

GLA_HEADS = max(4, D_MODEL // 512)
GLA_DK = D_MODEL // 2
GLA_DV = D_MODEL
GLA_HK = GLA_DK // GLA_HEADS
GLA_HV = GLA_DV // GLA_HEADS
GLA_GATE_RANK = 16
GLA_GATE_TAU = 16.0
GLA_CHUNK = 64

FOX_HD = 128
FOX_HEADS = D_MODEL // FOX_HD
FOX_BLOCK = 128

D_FF = 256 * ((8 * D_MODEL // 3 + 255) // 256)
CONV_W = 3

N_MOD = 6
NORM_EPS = 1e-6

kernel_name = 'yoco_gla_fox_adaln_convffn'


def rms_norm(x, gain):
    xf = x.astype(jnp.float32)
    y = xf * lax.rsqrt(jnp.mean(xf * xf, axis=-1, keepdims=True) + NORM_EPS)
    return (y * gain.astype(jnp.float32)).astype(x.dtype)


def modulate(h, shift, scale):
    return h * (1 + scale[:, None, :]) + shift[:, None, :]


def gla_mixer(h, w_in, w_gate_up, b_gate, b_r, head_g, w_o):
    B, S, _ = h.shape
    n_chunks = S // GLA_CHUNK
    proj = h @ w_in
    q, k, v, r, g_low = jnp.split(
        proj, [GLA_DK, 2 * GLA_DK, 2 * GLA_DK + GLA_DV, 2 * GLA_DK + 2 * GLA_DV], axis=-1)
    log_a = jax.nn.log_sigmoid((g_low @ w_gate_up + b_gate).astype(jnp.float32)) / GLA_GATE_TAU

    def chunks(t, hd):
        return t.astype(jnp.float32).reshape(B, n_chunks, GLA_CHUNK, GLA_HEADS, hd).transpose(0, 3, 1, 2, 4)

    q = chunks(q, GLA_HK) * GLA_HK ** -0.5
    k = chunks(k, GLA_HK)
    v = chunks(v, GLA_HV)
    b = jnp.cumsum(chunks(log_a, GLA_HK), axis=3)
    b_last = b[:, :, :, -1:, :]
    q_dec = q * jnp.exp(b)
    k_inv = k * jnp.exp(-b)
    k_dec = k * jnp.exp(b_last - b)
    causal = jnp.tril(jnp.ones((GLA_CHUNK, GLA_CHUNK), dtype=bool))
    scores = jnp.where(causal, jnp.einsum('bhntd,bhnsd->bhnts', q_dec, k_inv), 0.0)
    o_intra = jnp.einsum('bhnts,bhnsv->bhntv', scores, v)

    def step(state, xs):
        q_c, k_c, v_c, decay_c = xs
        o_c = jnp.einsum('bhtd,bhdv->bhtv', q_c, state)
        state = decay_c[..., None] * state + jnp.einsum('bhsd,bhsv->bhdv', k_c, v_c)
        return state, o_c

    xs = (jnp.moveaxis(q_dec, 2, 0), jnp.moveaxis(k_dec, 2, 0), jnp.moveaxis(v, 2, 0),
          jnp.moveaxis(jnp.exp(b_last[:, :, :, 0, :]), 2, 0))
    state0 = jnp.zeros((B, GLA_HEADS, GLA_HK, GLA_HV), jnp.float32)
    _, o_inter = lax.scan(step, state0, xs)
    o = o_intra + jnp.moveaxis(o_inter, 0, 2)
    o = o.transpose(0, 2, 3, 1, 4).reshape(B, S, GLA_HEADS, GLA_HV)
    gate = jax.nn.silu((r + b_r).astype(jnp.float32)).reshape(B, S, GLA_HEADS, GLA_HV)
    o = rms_norm(o, head_g) * gate
    return o.reshape(B, S, GLA_DV).astype(h.dtype) @ w_o


def fox_shared_kv(u, w_kv, b_f):
    B, S, _ = u.shape
    proj = u @ w_kv
    k, v, f_logit = jnp.split(proj, [D_MODEL, 2 * D_MODEL], axis=-1)
    k = k.reshape(B, S, FOX_HEADS, FOX_HD).transpose(0, 2, 1, 3)
    v = v.reshape(B, S, FOX_HEADS, FOX_HD).transpose(0, 2, 1, 3)
    log_f = jax.nn.log_sigmoid(f_logit.astype(jnp.float32) + b_f)
    cum = jnp.cumsum(log_f, axis=1).transpose(0, 2, 1)
    return k, v, cum


def fox_attention(h, w_q, w_o, k, v, cum):
    B, S, _ = h.shape
    q = (h @ w_q).reshape(B, S, FOX_HEADS, FOX_HD).transpose(0, 2, 1, 3) * FOX_HD ** -0.5
    outs = []
    for i in range(S // FOX_BLOCK):
        q0 = i * FOX_BLOCK
        q1 = q0 + FOX_BLOCK
        kb = k[:, :, :q1]
        vb = v[:, :, :q1]
        logits = jnp.einsum('bhtd,bhsd->bhts', q[:, :, q0:q1], kb).astype(jnp.float32)
        logits = logits + cum[:, :, q0:q1, None] - cum[:, :, None, :q1]
        mask = (q0 + jnp.arange(FOX_BLOCK))[:, None] >= jnp.arange(q1)[None, :]
        p = jax.nn.softmax(jnp.where(mask, logits, -jnp.inf), axis=-1)
        outs.append(jnp.einsum('bhts,bhsd->bhtd', p.astype(vb.dtype), vb))
    o = jnp.concatenate(outs, axis=2)
    return o.transpose(0, 2, 1, 3).reshape(B, S, D_MODEL) @ w_o


def conv_ffn(h, w_up, conv_w, conv_b, w_down):
    u = h @ w_up
    u = lax.conv_general_dilated(
        u, conv_w[:, None, :].astype(u.dtype), window_strides=(1,), padding=[(CONV_W - 1, 0)],
        dimension_numbers=('NWC', 'WIO', 'NWC'), feature_group_count=2 * D_FF) + conv_b
    a, g = jnp.split(u, 2, axis=-1)
    return (jax.nn.silu(a) * g) @ w_down


def setup_inputs(seed: int = 0) -> dict:
    key = jax.random.key(seed)
    ks = jax.random.split(key, 24)
    f32 = jnp.float32
    D = D_MODEL

    def nrm(k, shape, scale):
        return jax.random.normal(k, shape, f32) * scale

    gla_cols = 2 * GLA_DK + 2 * GLA_DV + GLA_GATE_RANK
    return {
        'x': nrm(ks[0], (BATCH, SEQ, D), 1.0),
        'c': nrm(ks[1], (BATCH, D), 1.0),
        'ada_w': nrm(ks[2], (DEPTH, D, N_MOD * D), 0.5 * D ** -0.5),
        'ada_b': nrm(ks[3], (DEPTH, N_MOD * D), 0.02),
        'norm1_g': 1.0 + nrm(ks[4], (DEPTH, D), 0.02),
        'norm2_g': 1.0 + nrm(ks[5], (DEPTH, D), 0.02),
        'gla_w_in': nrm(ks[6], (N_A_LAYERS, D, gla_cols), D ** -0.5),
        'gla_w_gate_up': nrm(ks[7], (N_A_LAYERS, GLA_GATE_RANK, GLA_DK), GLA_GATE_RANK ** -0.5),
        'gla_b_gate': nrm(ks[8], (N_A_LAYERS, GLA_DK), 0.1),
        'gla_b_r': nrm(ks[9], (N_A_LAYERS, GLA_DV), 0.02),
        'gla_head_g': 1.0 + nrm(ks[10], (N_A_LAYERS, GLA_HV), 0.02),
        'gla_w_o': nrm(ks[11], (N_A_LAYERS, GLA_DV, D), GLA_DV ** -0.5),
        'kv_ada_w': nrm(ks[12], (D, 2 * D), 0.5 * D ** -0.5),
        'kv_ada_b': nrm(ks[13], (2 * D,), 0.02),
        'kv_norm_g': 1.0 + nrm(ks[14], (D,), 0.02),
        'kv_w': nrm(ks[15], (D, 2 * D + FOX_HEADS), D ** -0.5),
        'kv_b_f': 2.0 + nrm(ks[16], (FOX_HEADS,), 0.1),
        'fox_w_q': nrm(ks[17], (N_B_LAYERS, D, D), D ** -0.5),
        'fox_w_o': nrm(ks[18], (N_B_LAYERS, D, D), D ** -0.5),
        'ffn_w_up': nrm(ks[19], (DEPTH, D, 2 * D_FF), D ** -0.5),
        'ffn_conv_w': nrm(ks[20], (DEPTH, CONV_W, 2 * D_FF), CONV_W ** -0.5),
        'ffn_conv_b': nrm(ks[21], (DEPTH, 2 * D_FF), 0.02),
        'ffn_w_down': nrm(ks[22], (DEPTH, D_FF, D), D_FF ** -0.5),
        'final_g': 1.0 + nrm(ks[23], (D,), 0.02),
    }


def reference(x, c, ada_w, ada_b, norm1_g, norm2_g, gla_w_in, gla_w_gate_up, gla_b_gate,
              gla_b_r, gla_head_g, gla_w_o, kv_ada_w, kv_ada_b, kv_norm_g, kv_w, kv_b_f,
              fox_w_q, fox_w_o, ffn_w_up, ffn_conv_w, ffn_conv_b, ffn_w_down, final_g):
    c_act = jax.nn.silu(c)
    shared = None
    for layer in range(DEPTH):
        sh1, sc1, gt1, sh2, sc2, gt2 = jnp.split(c_act @ ada_w[layer] + ada_b[layer], N_MOD, axis=-1)
        if layer == N_A_LAYERS:
            kv_sh, kv_sc = jnp.split(c_act @ kv_ada_w + kv_ada_b, 2, axis=-1)
            shared = fox_shared_kv(modulate(rms_norm(x, kv_norm_g), kv_sh, kv_sc), kv_w, kv_b_f)
        h = modulate(rms_norm(x, norm1_g[layer]), sh1, sc1)
        if layer < N_A_LAYERS:
            mix = gla_mixer(h, gla_w_in[layer], gla_w_gate_up[layer], gla_b_gate[layer],
                            gla_b_r[layer], gla_head_g[layer], gla_w_o[layer])
        else:
            j = layer - N_A_LAYERS
            k_sh, v_sh, cum_sh = shared
            mix = fox_attention(h, fox_w_q[j], fox_w_o[j], k_sh, v_sh, cum_sh)
        x = x + gt1[:, None, :] * mix.astype(x.dtype)
        h = modulate(rms_norm(x, norm2_g[layer]), sh2, sc2)
        ffn = conv_ffn(h, ffn_w_up[layer], ffn_conv_w[layer], ffn_conv_b[layer], ffn_w_down[layer])
        x = x + gt2[:, None, :] * ffn.astype(x.dtype)
    return rms_norm(x, final_g)
```

```python
import functools

import jax
import jax.numpy as jnp
from jax import lax
from jax.experimental import pallas as pl
from jax.experimental.pallas import tpu as pltpu

BF16 = jnp.bfloat16
F32 = jnp.float32

NORM_EPS = 1e-6
GLA_CHUNK = 64
GLA_GATE_TAU = 16.0
FOX_HD = 128
FOX_BLOCK = 128

LANE = 128
SUBLANE = 8
VMEM_LIMIT_BYTES = 56 * 1024 * 1024


def _params(*sem):
    return pltpu.CompilerParams(dimension_semantics=sem, vmem_limit_bytes=VMEM_LIMIT_BYTES)


def _pick(n, pref):
    t = min(n, pref)
    while n % t:
        t //= 2
    return t


def _silu(x):
    return x * jax.nn.sigmoid(x)


def _log_sigmoid(x):
    return jnp.minimum(x, 0.0) - jnp.log1p(jnp.exp(-jnp.abs(x)))


def _split_bf16(x, pieces):
    out = []
    rem = x
    for _ in range(pieces):
        p = rem.astype(BF16)
        out.append(p)
        rem = rem - p.astype(F32)
    return out


def _dot(a, b):
    return jnp.dot(a, b, preferred_element_type=F32)


def _dot_exact_lhs(a_bf16, x, pieces):
    acc = None
    for p in _split_bf16(x, pieces):
        t = _dot(a_bf16, p)
        acc = t if acc is None else acc + t
    return acc


def _dot3(a, b):
    a1, a2 = _split_bf16(a, 2)
    b1, b2 = _split_bf16(b, 2)
    return _dot(a1, b1) + (_dot(a1, b2) + _dot(a2, b1))


def _norm_mod(x, g, shift, scale):
    ms = jnp.mean(x * x, axis=-1, keepdims=True)
    y = (x * lax.rsqrt(ms + NORM_EPS)) * g
    return y * (1.0 + scale) + shift


def _ada_kernel(c_ref, w_ref, b_ref, o_ref):
    ca = _silu(c_ref[...]).astype(BF16)
    o_ref[0] = _dot(ca, w_ref[0].astype(BF16)) + b_ref[0]


def _ada(c, w, b):
    n_layers, d, n = w.shape
    bsz = c.shape[0]
    tn = _pick(n, 512)
    return pl.pallas_call(
        _ada_kernel,
        grid=(n_layers, n // tn),
        in_specs=[
            pl.BlockSpec((bsz, d), lambda l, j: (0, 0)),
            pl.BlockSpec((1, d, tn), lambda l, j: (l, 0, j)),
            pl.BlockSpec((1, 1, tn), lambda l, j: (l, 0, j)),
        ],
        out_specs=pl.BlockSpec((1, bsz, tn), lambda l, j: (l, 0, j)),
        out_shape=jax.ShapeDtypeStruct((n_layers, bsz, n), F32),
        compiler_params=_params("arbitrary", "arbitrary"),
        name="ada_mod",
    )(c, w, b.reshape(n_layers, 1, n))


def _nm_matmul_kernel(*refs, out_scale, has_small):
    if has_small:
        x_ref, g_ref, sh_ref, sc_ref, w_ref, ws_ref, o_ref, os_ref, h_ref = refs
    else:
        x_ref, g_ref, sh_ref, sc_ref, w_ref, o_ref, h_ref = refs

    @pl.when(pl.program_id(2) == 0)
    def _():
        hb = _norm_mod(x_ref[0], g_ref[...], sh_ref[0], sc_ref[0]).astype(BF16)
        h_ref[...] = hb
        if has_small:
            os_ref[0] = _dot(hb, ws_ref[...])

    acc = _dot(h_ref[...], w_ref[...])
    if out_scale != 1.0:
        acc = acc * out_scale
    o_ref[0] = acc.astype(o_ref.dtype)


def _nm_matmul(x, g, shift, scale, w, w_small=None, *, out_scale=1.0, tm=512, tn=1024):
    bsz, s, d = x.shape
    n = w.shape[1]
    tm = _pick(s, tm)
    tn = _pick(n, tn)
    has_small = w_small is not None
    in_specs = [
        pl.BlockSpec((1, tm, d), lambda b, i, j: (b, i, 0)),
        pl.BlockSpec((1, d), lambda b, i, j: (0, 0)),
        pl.BlockSpec((1, 1, d), lambda b, i, j: (b, 0, 0)),
        pl.BlockSpec((1, 1, d), lambda b, i, j: (b, 0, 0)),
        pl.BlockSpec((d, tn), lambda b, i, j: (0, j)),
    ]
    args = [x, g.reshape(1, d), shift, scale, w]
    out_specs = [pl.BlockSpec((1, tm, tn), lambda b, i, j: (b, i, j))]
    out_shape = [jax.ShapeDtypeStruct((bsz, s, n), BF16)]
    if has_small:
        ns = w_small.shape[1]
        in_specs.append(pl.BlockSpec((d, ns), lambda b, i, j: (0, 0)))
        args.append(w_small)
        out_specs.append(pl.BlockSpec((1, tm, ns), lambda b, i, j: (b, i, 0)))
        out_shape.append(jax.ShapeDtypeStruct((bsz, s, ns), F32))
    res = pl.pallas_call(
        functools.partial(_nm_matmul_kernel, out_scale=out_scale, has_small=has_small),
        grid=(bsz, s // tm, n // tn),
        in_specs=in_specs,
        out_specs=out_specs,
        out_shape=out_shape,
        scratch_shapes=[pltpu.VMEM((tm, d), BF16)],
        compiler_params=_params("arbitrary", "arbitrary", "arbitrary"),
        name="norm_mod_proj",
    )(*args)
    return res if has_small else res[0]


def _matmul_res_kernel(a_ref, w_ref, x_ref, gt_ref, o_ref):
    o_ref[0] = x_ref[0] + gt_ref[0] * _dot(a_ref[0], w_ref[...])


def _matmul_res(a, w, x, gate, *, tm=1024, tn=512):
    bsz, s, k = a.shape
    n = w.shape[1]
    tm = _pick(s, tm)
    tn = _pick(n, tn)
    return pl.pallas_call(
        _matmul_res_kernel,
        grid=(bsz, s // tm, n // tn),
        in_specs=[
            pl.BlockSpec((1, tm, k), lambda b, i, j: (b, i, 0)),
            pl.BlockSpec((k, tn), lambda b, i, j: (0, j)),
            pl.BlockSpec((1, tm, tn), lambda b, i, j: (b, i, j)),
            pl.BlockSpec((1, 1, tn), lambda b, i, j: (b, 0, j)),
        ],
        out_specs=pl.BlockSpec((1, tm, tn), lambda b, i, j: (b, i, j)),
        out_shape=jax.ShapeDtypeStruct((bsz, s, n), F32),
        compiler_params=_params("arbitrary", "arbitrary", "arbitrary"),
        name="proj_residual",
    )(a, w, x, gate)


def _gla_kernel(q_ref, k_ref, v_ref, r_ref, gl_ref, wgu_ref, bg_ref, br_ref, hg_ref,
                o_ref, la_ref, st_ref, *, chunk):
    s, hk = la_ref.shape
    n_chunks = s // chunk
    q_scale = float(hk) ** -0.5

    z = _dot3(gl_ref[0], wgu_ref[...]) + bg_ref[...]
    la_ref[...] = _log_sigmoid(z) / GLA_GATE_TAU
    st_ref[...] = jnp.zeros_like(st_ref)

    row = lax.broadcasted_iota(jnp.int32, (chunk, chunk), 0)
    col = lax.broadcasted_iota(jnp.int32, (chunk, chunk), 1)
    causal = row >= col
    tri = jnp.where(causal, 1.0, 0.0).astype(BF16)
    contract_last = (((1,), (1,)), ((), ()))
    contract_first = (((0,), (0,)), ((), ()))

    def body(c, carry):
        r0 = pl.multiple_of(c * chunk, chunk)
        rows = pl.ds(r0, chunk)
        b = _dot_exact_lhs(tri, la_ref[rows, :], 3)
        b_last = b[chunk - 1:chunk, :]
        q = q_ref[0, rows, :].astype(F32) * q_scale
        k = k_ref[0, rows, :].astype(F32)
        v = v_ref[0, rows, :]
        q_dec = (q * jnp.exp(b)).astype(BF16)
        k_inv = (k * jnp.exp(-b)).astype(BF16)
        k_dec = (k * jnp.exp(b_last - b)).astype(BF16)
        scores = lax.dot_general(q_dec, k_inv, contract_last, preferred_element_type=F32)
        scores = jnp.where(causal, scores, 0.0).astype(BF16)
        st = st_ref[...]
        o = _dot(scores, v) + lax.dot_general(q_dec, st.astype(BF16), contract_last,
                                              preferred_element_type=F32)
        st_ref[...] = st * jnp.exp(b_last) + lax.dot_general(
            v, k_dec, contract_first, preferred_element_type=F32)
        ms = jnp.mean(o * o, axis=-1, keepdims=True)
        o = (o * lax.rsqrt(ms + NORM_EPS)) * hg_ref[...]
        gate = _silu(r_ref[0, rows, :].astype(F32) + br_ref[...])
        o_ref[0, rows, :] = (o * gate).astype(o_ref.dtype)
        return carry

    lax.fori_loop(0, n_chunks, body, 0)


def _gla(proj, g_low, w_gate_up, b_gate, b_r, head_g, *, heads, dk, dv):
    bsz, s, _ = proj.shape
    hk, hv = dk // heads, dv // heads
    nl = g_low.shape[-1]
    k_blk = dk // hk
    v_blk = (2 * dk) // hv
    r_blk = (2 * dk + dv) // hv
    assert (2 * dk) % hv == 0 and s % GLA_CHUNK == 0
    return pl.pallas_call(
        functools.partial(_gla_kernel, chunk=GLA_CHUNK),
        grid=(bsz, heads),
        in_specs=[
            pl.BlockSpec((1, s, hk), lambda b, h: (b, 0, h)),
            pl.BlockSpec((1, s, hk), lambda b, h: (b, 0, k_blk + h)),
            pl.BlockSpec((1, s, hv), lambda b, h: (b, 0, v_blk + h)),
            pl.BlockSpec((1, s, hv), lambda b, h: (b, 0, r_blk + h)),
            pl.BlockSpec((1, s, nl), lambda b, h: (b, 0, 0)),
            pl.BlockSpec((nl, hk), lambda b, h: (0, h)),
            pl.BlockSpec((1, hk), lambda b, h: (0, h)),
            pl.BlockSpec((1, hv), lambda b, h: (0, h)),
            pl.BlockSpec((1, hv), lambda b, h: (0, 0)),
        ],
        out_specs=pl.BlockSpec((1, s, hv), lambda b, h: (b, 0, h)),
        out_shape=jax.ShapeDtypeStruct((bsz, s, dv), BF16),
        scratch_shapes=[pltpu.VMEM((s, hk), F32), pltpu.VMEM((hv, hk), F32)],
        compiler_params=_params("arbitrary", "arbitrary"),
        name="gla_mixer",
    )(proj, proj, proj, proj, g_low, w_gate_up, b_gate.reshape(1, dk), b_r.reshape(1, dv),
      head_g.reshape(1, hv))


def _fox_cum_kernel(f_ref, bf_ref, o_ref, cum_ref, *, blk):
    s = f_ref.shape[1]
    row = lax.broadcasted_iota(jnp.int32, (blk, blk), 0)
    col = lax.broadcasted_iota(jnp.int32, (blk, blk), 1)
    tri = jnp.where(row >= col, 1.0, 0.0).astype(BF16)
    carry = jnp.zeros((1, f_ref.shape[2]), F32)
    for i in range(s // blk):
        rows = slice(i * blk, (i + 1) * blk)
        lf = _log_sigmoid(f_ref[0, rows, :] + bf_ref[...])
        cum = _dot_exact_lhs(tri, lf, 3) + carry
        cum_ref[rows, :] = cum
        carry = cum[blk - 1:blk, :]
    o_ref[0] = cum_ref[...].T


def _fox_cum(f_logit, b_f):
    bsz, s, nl = f_logit.shape
    blk = _pick(s, 256)
    return pl.pallas_call(
        functools.partial(_fox_cum_kernel, blk=blk),
        grid=(bsz,),
        in_specs=[
            pl.BlockSpec((1, s, nl), lambda b: (b, 0, 0)),
            pl.BlockSpec((1, nl), lambda b: (0, 0)),
        ],
        out_specs=pl.BlockSpec((1, nl, s), lambda b: (b, 0, 0)),
        out_shape=jax.ShapeDtypeStruct((bsz, nl, s), F32),
        scratch_shapes=[pltpu.VMEM((s, nl), F32)],
        compiler_params=_params("arbitrary"),
        name="fox_cum",
    )(f_logit, b_f)


def _fox_attn_kernel(q_ref, k_ref, v_ref, cum_ref, o_ref, *, blk):
    s = q_ref.shape[1]
    row = lax.broadcasted_iota(jnp.int32, (blk, blk), 0)
    col = lax.broadcasted_iota(jnp.int32, (blk, blk), 1)
    diag = row == col
    contract_last = (((1,), (1,)), ((), ()))
    for i in range(s // blk):
        q0, q1 = i * blk, (i + 1) * blk
        logits = lax.dot_general(q_ref[0, q0:q1, :], k_ref[0, :q1, :], contract_last,
                                 preferred_element_type=F32)
        cum_k = cum_ref[0, 0, :, :q1]
        cum_q = jnp.sum(jnp.where(diag, cum_ref[0, 0, :, q0:q1], 0.0), axis=1, keepdims=True)
        logits = logits + cum_q - cum_k
        key_pos = lax.broadcasted_iota(jnp.int32, (blk, q1), 1)
        qry_pos = q0 + lax.broadcasted_iota(jnp.int32, (blk, q1), 0)
        logits = jnp.where(qry_pos >= key_pos, logits, -jnp.inf)
        m = jnp.max(logits, axis=-1, keepdims=True)
        p = jnp.exp(logits - m)
        denom = jnp.sum(p, axis=-1, keepdims=True)
        o = _dot(p.astype(BF16), v_ref[0, :q1, :])
        o_ref[0, q0:q1, :] = (o / denom).astype(o_ref.dtype)


def _fox_attn(q, kv, cum, *, heads):
    bsz, s, d = q.shape
    hd = d // heads
    return pl.pallas_call(
        functools.partial(_fox_attn_kernel, blk=FOX_BLOCK),
        grid=(bsz, heads),
        in_specs=[
            pl.BlockSpec((1, s, hd), lambda b, h: (b, 0, h)),
            pl.BlockSpec((1, s, hd), lambda b, h: (b, 0, h)),
            pl.BlockSpec((1, s, hd), lambda b, h: (b, 0, heads + h)),
            pl.BlockSpec((1, 1, 1, s), lambda b, h: (b, h, 0, 0)),
        ],
        out_specs=pl.BlockSpec((1, s, hd), lambda b, h: (b, 0, h)),
        out_shape=jax.ShapeDtypeStruct((bsz, s, d), BF16),
        compiler_params=_params("arbitrary", "arbitrary"),
        name="fox_attention",
    )(q, kv, kv, cum)


def _ffn_kernel(*refs, final_norm):
    if final_norm:
        (x_ref, g_ref, sh_ref, sc_ref, gt_ref, wa_ref, wg_ref, cwa_ref, cwg_ref, cba_ref,
         cbg_ref, wd_ref, fg_ref, o_ref, h_ref, ua_ref, ug_ref, ta_ref, tg_ref) = refs
    else:
        (x_ref, g_ref, sh_ref, sc_ref, gt_ref, wa_ref, wg_ref, cwa_ref, cwg_ref, cba_ref,
         cbg_ref, wd_ref, o_ref, h_ref, ua_ref, ug_ref, ta_ref, tg_ref) = refs
    i = pl.program_id(1)
    f = pl.program_id(2)
    nf = pl.num_programs(2)
    tm = x_ref.shape[1]
    halo = SUBLANE

    @pl.when(f == 0)
    def _():
        h_ref[...] = _norm_mod(x_ref[0], g_ref[...], sh_ref[0], sc_ref[0]).astype(BF16)

    h = h_ref[...]
    ua = _dot(h, wa_ref[...])
    ug = _dot(h, wg_ref[...])

    @pl.when(i == 0)
    def _():
        ua_ref[0:halo, :] = jnp.zeros((halo, ua_ref.shape[1]), F32)
        ug_ref[0:halo, :] = jnp.zeros((halo, ug_ref.shape[1]), F32)

    @pl.when(i > 0)
    def _():
        ua_ref[0:halo, :] = ta_ref[f]
        ug_ref[0:halo, :] = tg_ref[f]

    ua_ref[halo:halo + tm, :] = ua
    ug_ref[halo:halo + tm, :] = ug
    ta_ref[f] = ua[tm - halo:tm, :]
    tg_ref[f] = ug[tm - halo:tm, :]

    def conv(u_ref, cw_ref, cb_ref):
        acc = cw_ref[0:1, :] * u_ref[halo - 2:halo - 2 + tm, :]
        acc = acc + cw_ref[1:2, :] * u_ref[halo - 1:halo - 1 + tm, :]
        acc = acc + cw_ref[2:3, :] * u_ref[halo:halo + tm, :]
        return acc + cb_ref[...]

    a = conv(ua_ref, cwa_ref, cba_ref)
    g = conv(ug_ref, cwg_ref, cbg_ref)
    part = _dot((_silu(a) * g).astype(BF16), wd_ref[...])

    @pl.when(f == 0)
    def _():
        o_ref[0] = part

    @pl.when(f > 0)
    def _():
        o_ref[0] += part

    @pl.when(f == nf - 1)
    def _():
        y = x_ref[0] + gt_ref[0] * o_ref[0]
        if final_norm:
            ms = jnp.mean(y * y, axis=-1, keepdims=True)
            y = (y * lax.rsqrt(ms + NORM_EPS)) * fg_ref[...]
        o_ref[0] = y


def _ffn(x, g, shift, scale, gate, w_up, conv_w, conv_b, w_down, final_g=None, *, tm=512, tf=256):
    bsz, s, d = x.shape
    ff = w_down.shape[0]
    tm = _pick(s, tm)
    tf = _pick(ff, tf)
    nf = ff // tf
    assert conv_w.shape[0] == 3
    final_norm = final_g is not None
    once = pl.Buffered(1)
    in_specs = [
        pl.BlockSpec((1, tm, d), lambda b, i, f: (b, i, 0), pipeline_mode=once),
        pl.BlockSpec((1, d), lambda b, i, f: (0, 0)),
        pl.BlockSpec((1, 1, d), lambda b, i, f: (b, 0, 0)),
        pl.BlockSpec((1, 1, d), lambda b, i, f: (b, 0, 0)),
        pl.BlockSpec((1, 1, d), lambda b, i, f: (b, 0, 0)),
        pl.BlockSpec((d, tf), lambda b, i, f: (0, f)),
        pl.BlockSpec((d, tf), lambda b, i, f: (0, nf + f)),
        pl.BlockSpec((3, tf), lambda b, i, f: (0, f)),
        pl.BlockSpec((3, tf), lambda b, i, f: (0, nf + f)),
        pl.BlockSpec((1, tf), lambda b, i, f: (0, f)),
        pl.BlockSpec((1, tf), lambda b, i, f: (0, nf + f)),
        pl.BlockSpec((tf, d), lambda b, i, f: (f, 0)),
    ]
    cb = conv_b.reshape(1, 2 * ff)
    args = [x, g.reshape(1, d), shift, scale, gate, w_up, w_up, conv_w, conv_w, cb, cb, w_down]
    if final_norm:
        in_specs.append(pl.BlockSpec((1, d), lambda b, i, f: (0, 0)))
        args.append(final_g.reshape(1, d))
    return pl.pallas_call(
        functools.partial(_ffn_kernel, final_norm=final_norm),
        grid=(bsz, s // tm, nf),
        in_specs=in_specs,
        out_specs=pl.BlockSpec((1, tm, d), lambda b, i, f: (b, i, 0), pipeline_mode=once),
        out_shape=jax.ShapeDtypeStruct((bsz, s, d), F32),
        scratch_shapes=[
            pltpu.VMEM((tm, d), BF16),
            pltpu.VMEM((tm + SUBLANE, tf), F32),
            pltpu.VMEM((tm + SUBLANE, tf), F32),
            pltpu.VMEM((nf, SUBLANE, tf), F32),
            pltpu.VMEM((nf, SUBLANE, tf), F32),
        ],
        compiler_params=_params("arbitrary", "arbitrary", "arbitrary"),
        name="conv_ffn",
    )(*args)


def _pad_cols(w, n):
    return jnp.pad(w, ((0, 0), (0, n - w.shape[1])))


def kernel(x, c, ada_w, ada_b, norm1_g, norm2_g, gla_w_in, gla_w_gate_up, gla_b_gate, gla_b_r,
           gla_head_g, gla_w_o, kv_ada_w, kv_ada_b, kv_norm_g, kv_w, kv_b_f, fox_w_q, fox_w_o,
           ffn_w_up, ffn_conv_w, ffn_conv_b, ffn_w_down, final_g):
    bsz, s, d = x.shape
    depth = ada_w.shape[0]
    n_a = gla_w_in.shape[0]
    dk = gla_w_gate_up.shape[2]
    dv = gla_b_r.shape[1]
    gla_heads = dv // gla_head_g.shape[1]
    gate_rank = gla_w_gate_up.shape[1]
    fox_heads = kv_b_f.shape[0]

    mods = _ada(c, ada_w, ada_b)
    kv_mod = _ada(c, kv_ada_w[None], kv_ada_b[None])[0]

    def chunks(m, n):
        return [m[:, None, k * d:(k + 1) * d] for k in range(n)]

    for layer in range(depth):
        sh1, sc1, gt1, sh2, sc2, gt2 = chunks(mods[layer], 6)
        if layer < n_a:
            w_in = gla_w_in[layer]
            n_main = 2 * dk + 2 * dv
            proj, g_low = _nm_matmul(
                x, norm1_g[layer], sh1, sc1, w_in[:, :n_main].astype(BF16),
                _pad_cols(w_in[:, n_main:], LANE).astype(BF16))
            w_gu = jnp.pad(gla_w_gate_up[layer], ((0, LANE - gate_rank), (0, 0)))
            mix = _gla(proj, g_low, w_gu, gla_b_gate[layer], gla_b_r[layer], gla_head_g[layer],
                       heads=gla_heads, dk=dk, dv=dv)
            w_o = gla_w_o[layer]
        else:
            j = layer - n_a
            if layer == n_a:
                kv_sh, kv_sc = chunks(kv_mod, 2)
                kv, f_logit = _nm_matmul(
                    x, kv_norm_g, kv_sh, kv_sc, kv_w[:, :2 * d].astype(BF16),
                    _pad_cols(kv_w[:, 2 * d:], LANE).astype(BF16))
                cum = _fox_cum(f_logit, _pad_cols(kv_b_f[None, :], LANE))
                cum = cum[:, :fox_heads, None, :]
            q = _nm_matmul(x, norm1_g[layer], sh1, sc1, fox_w_q[j].astype(BF16),
                           out_scale=float(d // fox_heads) ** -0.5)
            mix = _fox_attn(q, kv, cum, heads=fox_heads)
            w_o = fox_w_o[j]
        x = _matmul_res(mix, w_o.astype(BF16), x, gt1)
        x = _ffn(x, norm2_g[layer], sh2, sc2, gt2, ffn_w_up[layer].astype(BF16),
                 ffn_conv_w[layer], ffn_conv_b[layer], ffn_w_down[layer].astype(BF16),
                 final_g if layer == depth - 1 else None)
    return x
```

```python
import functools
import math

import jax
import jax.numpy as jnp
from jax import lax
from jax.experimental import pallas as pl
from jax.experimental.pallas import tpu as pltpu

BF16 = jnp.bfloat16
F32 = jnp.float32

NORM_EPS = 1e-6
GLA_CHUNK = 64
GLA_GATE_TAU = 16.0
LOG2_E = math.log2(math.e)

LANE = 128
SUBLANE = 8
VMEM_LIMIT_BYTES = 56 * 1024 * 1024
NORM_ROWS = 2 * SUBLANE
NORM_UNROLL = 4

_NT = (((1,), (1,)), ((), ()))
_TN = (((0,), (0,)), ((), ()))


def _params(*sem):
    return pltpu.CompilerParams(dimension_semantics=sem, vmem_limit_bytes=VMEM_LIMIT_BYTES)


def _pick(n, pref):
    t = min(n, pref)
    while n % t:
        t //= 2
    return t


def _round_up(n, m):
    return (n + m - 1) // m * m


def _silu(x):
    return x * jax.nn.sigmoid(x)


def _log_sigmoid(x):
    return jnp.minimum(x, 0.0) - jnp.log1p(jnp.exp(-jnp.abs(x)))


def _split_bf16(x, pieces):
    out = []
    rem = x
    for _ in range(pieces):
        p = rem.astype(BF16)
        out.append(p)
        rem = rem - p.astype(F32)
    return out


def _dot(a, b):
    return jnp.dot(a, b, preferred_element_type=F32)


def _dot_exact_lhs(a_bf16, x, pieces):
    acc = None
    for p in _split_bf16(x, pieces):
        t = _dot(a_bf16, p)
        acc = t if acc is None else acc + t
    return acc


def _dot3(a, b):
    a1, a2 = _split_bf16(a, 2)
    b1, b2 = _split_bf16(b, 2)
    return _dot(a1, b1) + (_dot(a1, b2) + _dot(a2, b1))


def _rms_scale(x):
    return lax.rsqrt(jnp.mean(x * x, axis=-1, keepdims=True) + NORM_EPS)


def _norm_mod_to(h_ref, x_ref, g_ref, sh_ref, sc_ref):
    rows = x_ref.shape[1]
    step = _pick(rows, NORM_ROWS)
    g = g_ref[...]
    shift = sh_ref[0]
    scale1 = 1.0 + sc_ref[0]

    def body(r, carry):
        sl = pl.ds(pl.multiple_of(r * step, step), step)
        inv = _rms_scale(x_ref[0, sl, :])
        y = (x_ref[0, sl, :] * inv) * g
        h_ref[sl, :] = (y * scale1 + shift).astype(h_ref.dtype)
        return carry

    lax.fori_loop(0, rows // step, body, 0, unroll=NORM_UNROLL)


def _ada_kernel(c_ref, w_ref, b_ref, o_ref):
    ca = _silu(c_ref[...]).astype(BF16)
    o_ref[0] = _dot(ca, w_ref[0].astype(BF16)) + b_ref[0]


def _ada(c, w, b):
    n_layers, d, n = w.shape
    bsz = c.shape[0]
    tn = _pick(n, 512)
    return pl.pallas_call(
        _ada_kernel,
        grid=(n_layers, n // tn),
        in_specs=[
            pl.BlockSpec((bsz, d), lambda l, j: (0, 0)),
            pl.BlockSpec((1, d, tn), lambda l, j: (l, 0, j)),
            pl.BlockSpec((1, 1, tn), lambda l, j: (l, 0, j)),
        ],
        out_specs=pl.BlockSpec((1, bsz, tn), lambda l, j: (l, 0, j)),
        out_shape=jax.ShapeDtypeStruct((n_layers, bsz, n), F32),
        compiler_params=_params("arbitrary", "arbitrary"),
        name="ada_mod",
    )(c, w, b.reshape(n_layers, 1, n))


def _nm_matmul_kernel(*refs, out_scale, has_small):
    if has_small:
        x_ref, g_ref, sh_ref, sc_ref, w_ref, ws_ref, o_ref, os_ref, h_ref = refs
    else:
        x_ref, g_ref, sh_ref, sc_ref, w_ref, o_ref, h_ref = refs

    @pl.when(pl.program_id(2) == 0)
    def _():
        _norm_mod_to(h_ref, x_ref, g_ref, sh_ref, sc_ref)
        if has_small:
            os_ref[0] = _dot(h_ref[...], ws_ref[...])

    acc = _dot(h_ref[...], w_ref[...])
    if out_scale != 1.0:
        acc = acc * out_scale
    o_ref[0] = acc.astype(o_ref.dtype)


def _nm_matmul(x, g, shift, scale, w, layer, n_out, w_small=None, *, out_scale=1.0,
               tm=1024, tn=1024):
    bsz, s, d = x.shape
    tm = _pick(s, tm)
    tn = _pick(n_out, tn)
    has_small = w_small is not None
    in_specs = [
        pl.BlockSpec((1, tm, d), lambda b, i, j: (b, i, 0), pipeline_mode=pl.Buffered(1)),
        pl.BlockSpec((1, d), lambda b, i, j: (0, 0)),
        pl.BlockSpec((1, 1, d), lambda b, i, j: (b, 0, 0)),
        pl.BlockSpec((1, 1, d), lambda b, i, j: (b, 0, 0)),
        pl.BlockSpec((None, d, tn), lambda b, i, j: (layer, 0, j)),
    ]
    args = [x, g.reshape(1, d), shift, scale, w]
    out_specs = [pl.BlockSpec((1, tm, tn), lambda b, i, j: (b, i, j))]
    out_shape = [jax.ShapeDtypeStruct((bsz, s, n_out), BF16)]
    if has_small:
        ns = w_small.shape[1]
        in_specs.append(pl.BlockSpec((d, ns), lambda b, i, j: (0, 0)))
        args.append(w_small)
        out_specs.append(pl.BlockSpec((1, tm, ns), lambda b, i, j: (b, i, 0)))
        out_shape.append(jax.ShapeDtypeStruct((bsz, s, ns), F32))
    res = pl.pallas_call(
        functools.partial(_nm_matmul_kernel, out_scale=out_scale, has_small=has_small),
        grid=(bsz, s // tm, n_out // tn),
        in_specs=in_specs,
        out_specs=out_specs,
        out_shape=out_shape,
        scratch_shapes=[pltpu.VMEM((tm, d), BF16)],
        compiler_params=_params("arbitrary", "arbitrary", "arbitrary"),
        name="norm_mod_proj",
    )(*args)
    return res if has_small else res[0]


def _matmul_res_kernel(a_ref, w_ref, x_ref, gt_ref, o_ref):
    o_ref[0] = x_ref[0] + gt_ref[0] * _dot(a_ref[0], w_ref[...])


def _matmul_res(a, w, layer, x, gate, *, tm=1024, tn=1024):
    bsz, s, k = a.shape
    n = w.shape[2]
    tm = _pick(s, tm)
    tn = _pick(n, tn)
    return pl.pallas_call(
        _matmul_res_kernel,
        grid=(bsz, s // tm, n // tn),
        in_specs=[
            pl.BlockSpec((1, tm, k), lambda b, i, j: (b, i, 0)),
            pl.BlockSpec((None, k, tn), lambda b, i, j: (layer, 0, j)),
            pl.BlockSpec((1, tm, tn), lambda b, i, j: (b, i, j)),
            pl.BlockSpec((1, 1, tn), lambda b, i, j: (b, 0, j)),
        ],
        out_specs=pl.BlockSpec((1, tm, tn), lambda b, i, j: (b, i, j)),
        out_shape=jax.ShapeDtypeStruct((bsz, s, n), F32),
        compiler_params=_params("arbitrary", "arbitrary", "arbitrary"),
        name="proj_residual",
    )(a, w, x, gate)


def _gla_kernel(q_ref, k_ref, v_ref, r_ref, gl_ref, wgu_ref, bg_ref, br_ref, hg_ref,
                o_ref, la_ref, st_ref, *, chunk):
    s, hk = la_ref.shape
    n_chunks = s // chunk
    q_scale = float(hk) ** -0.5

    z = _dot3(gl_ref[0], wgu_ref[...]) + bg_ref[...]
    la_ref[...] = _log_sigmoid(z) / GLA_GATE_TAU
    st_ref[...] = jnp.zeros_like(st_ref)

    row = lax.broadcasted_iota(jnp.int32, (chunk, chunk), 0)
    col = lax.broadcasted_iota(jnp.int32, (chunk, chunk), 1)
    causal = row >= col
    tri = jnp.where(causal, 1.0, 0.0).astype(BF16)

    def body(c, carry):
        r0 = pl.multiple_of(c * chunk, chunk)
        rows = pl.ds(r0, chunk)
        b = _dot_exact_lhs(tri, la_ref[rows, :], 3)
        b_last = b[chunk - 1:chunk, :]
        q = q_ref[0, rows, :].astype(F32) * q_scale
        k = k_ref[0, rows, :].astype(F32)
        v = v_ref[0, rows, :]
        q_dec = (q * jnp.exp(b)).astype(BF16)
        k_inv = (k * jnp.exp(-b)).astype(BF16)
        k_dec = (k * jnp.exp(b_last - b)).astype(BF16)
        scores = lax.dot_general(q_dec, k_inv, _NT, preferred_element_type=F32)
        scores = jnp.where(causal, scores, 0.0).astype(BF16)
        st = st_ref[...]
        o = _dot(scores, v) + lax.dot_general(q_dec, st.astype(BF16), _NT,
                                              preferred_element_type=F32)
        st_ref[...] = st * jnp.exp(b_last) + lax.dot_general(
            v, k_dec, _TN, preferred_element_type=F32)
        o = (o * _rms_scale(o)) * hg_ref[...]
        gate = _silu(r_ref[0, rows, :].astype(F32) + br_ref[...])
        o_ref[0, rows, :] = (o * gate).astype(o_ref.dtype)
        return carry

    lax.fori_loop(0, n_chunks, body, 0, unroll=2)


def _gla(proj, g_low, w_gate_up, b_gate, b_r, head_g, *, heads, dk, dv):
    bsz, s, _ = proj.shape
    hk, hv = dk // heads, dv // heads
    nl = g_low.shape[-1]
    k_blk = dk // hk
    v_blk = (2 * dk) // hv
    r_blk = (2 * dk + dv) // hv
    assert (2 * dk) % hv == 0 and s % (2 * GLA_CHUNK) == 0
    return pl.pallas_call(
        functools.partial(_gla_kernel, chunk=GLA_CHUNK),
        grid=(bsz, heads),
        in_specs=[
            pl.BlockSpec((1, s, hk), lambda b, h: (b, 0, h)),
            pl.BlockSpec((1, s, hk), lambda b, h: (b, 0, k_blk + h)),
            pl.BlockSpec((1, s, hv), lambda b, h: (b, 0, v_blk + h)),
            pl.BlockSpec((1, s, hv), lambda b, h: (b, 0, r_blk + h)),
            pl.BlockSpec((1, s, nl), lambda b, h: (b, 0, 0)),
            pl.BlockSpec((nl, hk), lambda b, h: (0, h)),
            pl.BlockSpec((1, hk), lambda b, h: (0, h)),
            pl.BlockSpec((1, hv), lambda b, h: (0, h)),
            pl.BlockSpec((1, hv), lambda b, h: (0, 0)),
        ],
        out_specs=pl.BlockSpec((1, s, hv), lambda b, h: (b, 0, h)),
        out_shape=jax.ShapeDtypeStruct((bsz, s, dv), BF16),
        scratch_shapes=[pltpu.VMEM((s, hk), F32), pltpu.VMEM((hv, hk), F32)],
        compiler_params=_params("arbitrary", "arbitrary"),
        name="gla_mixer",
    )(proj, proj, proj, proj, g_low, w_gate_up, b_gate.reshape(1, dk), b_r.reshape(1, dv),
      head_g.reshape(1, hv))


def _fox_cum_kernel(f_ref, bf_ref, o_ref, cum_ref, *, blk):
    s = f_ref.shape[1]
    row = lax.broadcasted_iota(jnp.int32, (blk, blk), 0)
    col = lax.broadcasted_iota(jnp.int32, (blk, blk), 1)
    tri = jnp.where(row >= col, 1.0, 0.0).astype(BF16)
    carry = jnp.zeros((1, f_ref.shape[2]), F32)
    for i in range(s // blk):
        rows = slice(i * blk, (i + 1) * blk)
        lf = _log_sigmoid(f_ref[0, rows, :] + bf_ref[...])
        cum = _dot_exact_lhs(tri, lf, 3) + carry
        cum_ref[rows, :] = cum * LOG2_E
        carry = cum[blk - 1:blk, :]
    o_ref[0] = cum_ref[...].T


def _fox_cum(f_logit, b_f):
    bsz, s, nl = f_logit.shape
    blk = _pick(s, 256)
    return pl.pallas_call(
        functools.partial(_fox_cum_kernel, blk=blk),
        grid=(bsz,),
        in_specs=[
            pl.BlockSpec((1, s, nl), lambda b: (b, 0, 0)),
            pl.BlockSpec((1, nl), lambda b: (0, 0)),
        ],
        out_specs=pl.BlockSpec((1, nl, s), lambda b: (b, 0, 0)),
        out_shape=jax.ShapeDtypeStruct((bsz, nl, s), F32),
        scratch_shapes=[pltpu.VMEM((s, nl), F32)],
        compiler_params=_params("arbitrary"),
        name="fox_cum",
    )(f_logit, b_f)


def _lane_fold(x, op):
    out = x[:, :LANE]
    for c in range(1, x.shape[1] // LANE):
        out = op(out, x[:, c * LANE:(c + 1) * LANE])
    return out


def _fox_attn_kernel(q_ref, k_ref, v_ref, cum_ref, o_ref, t_ref, *, tq, tk):
    s, hd = q_ref.shape[1], q_ref.shape[2]
    eye = (lax.broadcasted_iota(jnp.int32, (tq, tq), 0)
           == lax.broadcasted_iota(jnp.int32, (tq, tq), 1))
    for i in range(s // tq):
        q0, q1 = i * tq, (i + 1) * tq
        q = q_ref[0, q0:q1, :]
        cum_q = jnp.sum(jnp.where(eye, cum_ref[0, 0, :, q0:q1], 0.0), axis=1, keepdims=True)
        chunks = [(k0, min(tk, q1 - k0)) for k0 in range(0, q1, tk)]
        mx = None
        for k0, w in chunks:
            t = lax.dot_general(q, k_ref[0, k0:k0 + w, :], _NT, preferred_element_type=F32)
            t = t - cum_ref[0, 0, :, k0:k0 + w]
            if k0 + w - 1 > q0:
                qpos = q0 + lax.broadcasted_iota(jnp.int32, (tq, w), 0)
                kpos = k0 + lax.broadcasted_iota(jnp.int32, (tq, w), 1)
                t = jnp.where(qpos >= kpos, t, -jnp.inf)
            t_ref[:, k0:k0 + w] = t
            cm = _lane_fold(t, jnp.maximum)
            mx = cm if mx is None else jnp.maximum(mx, cm)
        m = jnp.max(mx, axis=1, keepdims=True) + cum_q
        shift = cum_q - m
        den = jnp.zeros((tq, LANE), F32)
        acc = jnp.zeros((tq, hd), F32)
        for k0, w in chunks:
            p = jnp.exp2(t_ref[:, k0:k0 + w] + shift)
            den = den + _lane_fold(p, jnp.add)
            acc = acc + _dot(p.astype(BF16), v_ref[0, k0:k0 + w, :])
        o_ref[0, q0:q1, :] = (acc / jnp.sum(den, axis=1, keepdims=True)).astype(o_ref.dtype)


def _fox_attn(q, kv, cum, *, heads, tq=256, tk=256):
    bsz, s, d = q.shape
    hd = d // heads
    tq = _pick(s, tq)
    tk = _pick(s, tk)
    return pl.pallas_call(
        functools.partial(_fox_attn_kernel, tq=tq, tk=tk),
        grid=(bsz, heads),
        in_specs=[
            pl.BlockSpec((1, s, hd), lambda b, h: (b, 0, h)),
            pl.BlockSpec((1, s, hd), lambda b, h: (b, 0, h)),
            pl.BlockSpec((1, s, hd), lambda b, h: (b, 0, heads + h)),
            pl.BlockSpec((1, 1, 1, s), lambda b, h: (b, h, 0, 0)),
        ],
        out_specs=pl.BlockSpec((1, s, hd), lambda b, h: (b, 0, h)),
        out_shape=jax.ShapeDtypeStruct((bsz, s, d), BF16),
        scratch_shapes=[pltpu.VMEM((tq, s), F32)],
        compiler_params=_params("arbitrary", "arbitrary"),
        name="fox_attention",
    )(q, kv, kv, cum)


def _ffn_kernel(*refs, final_norm, n_sub):
    if final_norm:
        (x_ref, g_ref, sh_ref, sc_ref, gt_ref, wa_ref, wg_ref, cwa_ref, cwg_ref, cba_ref,
         cbg_ref, wd_ref, fg_ref, o_ref, h_ref, ua_ref, ug_ref, ta_ref, tg_ref) = refs
    else:
        (x_ref, g_ref, sh_ref, sc_ref, gt_ref, wa_ref, wg_ref, cwa_ref, cwg_ref, cba_ref,
         cbg_ref, wd_ref, o_ref, h_ref, ua_ref, ug_ref, ta_ref, tg_ref) = refs
    i = pl.program_id(1)
    f = pl.program_id(2)
    nf = pl.num_programs(2)
    tm = x_ref.shape[1]
    halo = SUBLANE

    @pl.when(f == 0)
    def _():
        _norm_mod_to(h_ref, x_ref, g_ref, sh_ref, sc_ref)
        o_ref[...] = jnp.zeros_like(o_ref)

    @pl.when(i == 0)
    def _():
        ua_ref[0:halo, :] = jnp.zeros((halo, ua_ref.shape[1]), F32)
        ug_ref[0:halo, :] = jnp.zeros((halo, ug_ref.shape[1]), F32)

    @pl.when(i > 0)
    def _():
        ua_ref[0:halo, :] = ta_ref[f]
        ug_ref[0:halo, :] = tg_ref[f]

    def conv(u_ref, cw_ref, cb_ref, r0):
        acc = cw_ref[0:1, :] * u_ref[r0 + halo - 2:r0 + halo - 2 + sub, :]
        acc = acc + cw_ref[1:2, :] * u_ref[r0 + halo - 1:r0 + halo - 1 + sub, :]
        acc = acc + cw_ref[2:3, :] * u_ref[r0 + halo:r0 + halo + sub, :]
        return acc + cb_ref[...]

    sub = tm // n_sub
    for r in range(n_sub):
        r0 = r * sub
        h = h_ref[r0:r0 + sub, :]
        ua = _dot(h, wa_ref[...])
        ug = _dot(h, wg_ref[...])
        ua_ref[halo + r0:halo + r0 + sub, :] = ua
        ug_ref[halo + r0:halo + r0 + sub, :] = ug
        if r == n_sub - 1:
            ta_ref[f] = ua[sub - halo:sub, :]
            tg_ref[f] = ug[sub - halo:sub, :]
    for r in range(n_sub):
        r0 = r * sub
        a = conv(ua_ref, cwa_ref, cba_ref, r0)
        g = conv(ug_ref, cwg_ref, cbg_ref, r0)
        o_ref[0, r0:r0 + sub, :] += _dot((_silu(a) * g).astype(BF16), wd_ref[...])

    @pl.when(f == nf - 1)
    def _():
        step = _pick(tm, NORM_ROWS)

        def body(r, carry):
            sl = pl.ds(pl.multiple_of(r * step, step), step)
            y = x_ref[0, sl, :] + gt_ref[0] * o_ref[0, sl, :]
            if final_norm:
                y = (y * _rms_scale(y)) * fg_ref[...]
            o_ref[0, sl, :] = y
            return carry

        lax.fori_loop(0, tm // step, body, 0, unroll=NORM_UNROLL)


def _ffn(x, g, shift, scale, gate, w_up, conv_w, conv_b, w_down, layer, final_g=None, *,
         tm=512, tf=512, n_sub=4):
    bsz, s, d = x.shape
    ff = w_down.shape[1]
    tm = _pick(s, tm)
    assert ff % tf == 0 and conv_w.shape[1] == 3
    nf = ff // tf
    final_norm = final_g is not None
    once = pl.Buffered(1)
    in_specs = [
        pl.BlockSpec((1, tm, d), lambda b, i, f: (b, i, 0), pipeline_mode=once),
        pl.BlockSpec((1, d), lambda b, i, f: (0, 0)),
        pl.BlockSpec((1, 1, d), lambda b, i, f: (b, 0, 0)),
        pl.BlockSpec((1, 1, d), lambda b, i, f: (b, 0, 0)),
        pl.BlockSpec((1, 1, d), lambda b, i, f: (b, 0, 0)),
        pl.BlockSpec((None, d, tf), lambda b, i, f: (layer, 0, f)),
        pl.BlockSpec((None, d, tf), lambda b, i, f: (layer, 0, nf + f)),
        pl.BlockSpec((None, 3, tf), lambda b, i, f: (layer, 0, f)),
        pl.BlockSpec((None, 3, tf), lambda b, i, f: (layer, 0, nf + f)),
        pl.BlockSpec((None, 1, tf), lambda b, i, f: (layer, 0, f)),
        pl.BlockSpec((None, 1, tf), lambda b, i, f: (layer, 0, nf + f)),
        pl.BlockSpec((None, tf, d), lambda b, i, f: (layer, f, 0)),
    ]
    args = [x, g.reshape(1, d), shift, scale, gate, w_up, w_up, conv_w, conv_w, conv_b, conv_b,
            w_down]
    if final_norm:
        in_specs.append(pl.BlockSpec((1, d), lambda b, i, f: (0, 0)))
        args.append(final_g.reshape(1, d))
    return pl.pallas_call(
        functools.partial(_ffn_kernel, final_norm=final_norm, n_sub=n_sub),
        grid=(bsz, s // tm, nf),
        in_specs=in_specs,
        out_specs=pl.BlockSpec((1, tm, d), lambda b, i, f: (b, i, 0), pipeline_mode=once),
        out_shape=jax.ShapeDtypeStruct((bsz, s, d), F32),
        scratch_shapes=[
            pltpu.VMEM((tm, d), BF16),
            pltpu.VMEM((tm + SUBLANE, tf), F32),
            pltpu.VMEM((tm + SUBLANE, tf), F32),
            pltpu.VMEM((nf, SUBLANE, tf), F32),
            pltpu.VMEM((nf, SUBLANE, tf), F32),
        ],
        compiler_params=_params("arbitrary", "arbitrary", "arbitrary"),
        name="conv_ffn",
    )(*args)


FFN_TF = 512


def _pad_last(w, n):
    return jnp.pad(w, [(0, 0)] * (w.ndim - 1) + [(0, n - w.shape[-1])])


def _pad_halves(w, ff, ffp):
    return jnp.concatenate([_pad_last(w[..., :ff], ffp), _pad_last(w[..., ff:], ffp)], axis=-1)


def kernel(x, c, ada_w, ada_b, norm1_g, norm2_g, gla_w_in, gla_w_gate_up, gla_b_gate, gla_b_r,
           gla_head_g, gla_w_o, kv_ada_w, kv_ada_b, kv_norm_g, kv_w, kv_b_f, fox_w_q, fox_w_o,
           ffn_w_up, ffn_conv_w, ffn_conv_b, ffn_w_down, final_g):
    bsz, s, d = x.shape
    depth = ada_w.shape[0]
    n_a = gla_w_in.shape[0]
    dk = gla_w_gate_up.shape[2]
    dv = gla_b_r.shape[1]
    gla_heads = dv // gla_head_g.shape[1]
    gate_rank = gla_w_gate_up.shape[1]
    fox_heads = kv_b_f.shape[0]
    ff = ffn_w_down.shape[1]
    ffp = _round_up(ff, FFN_TF)

    gla_w_in_b = gla_w_in.astype(BF16)
    gla_w_o_b = gla_w_o.astype(BF16)
    kv_w_b = kv_w.astype(BF16)[None]
    fox_w_q_b = fox_w_q.astype(BF16)
    fox_w_o_b = fox_w_o.astype(BF16)
    w_up_b = _pad_halves(ffn_w_up.astype(BF16), ff, ffp)
    w_down_b = jnp.pad(ffn_w_down.astype(BF16), ((0, 0), (0, ffp - ff), (0, 0)))
    conv_w_p = _pad_halves(ffn_conv_w, ff, ffp)
    conv_b_p = _pad_halves(ffn_conv_b, ff, ffp)[:, None, :]

    mods = _ada(c, ada_w, ada_b)
    kv_mod = _ada(c, kv_ada_w[None], kv_ada_b[None])[0]

    def chunks(m, n):
        return [m[:, None, k * d:(k + 1) * d] for k in range(n)]

    for layer in range(depth):
        sh1, sc1, gt1, sh2, sc2, gt2 = chunks(mods[layer], 6)
        if layer < n_a:
            n_main = 2 * dk + 2 * dv
            w_low = _pad_last(gla_w_in[layer][:, n_main:], LANE).astype(BF16)
            proj, g_low = _nm_matmul(x, norm1_g[layer], sh1, sc1, gla_w_in_b, layer, n_main, w_low)
            w_gu = jnp.pad(gla_w_gate_up[layer], ((0, LANE - gate_rank), (0, 0)))
            mix = _gla(proj, g_low, w_gu, gla_b_gate[layer], gla_b_r[layer], gla_head_g[layer],
                       heads=gla_heads, dk=dk, dv=dv)
            x = _matmul_res(mix, gla_w_o_b, layer, x, gt1)
        else:
            j = layer - n_a
            if layer == n_a:
                kv_sh, kv_sc = chunks(kv_mod, 2)
                w_low = _pad_last(kv_w[:, 2 * d:], LANE).astype(BF16)
                kv, f_logit = _nm_matmul(x, kv_norm_g, kv_sh, kv_sc, kv_w_b, 0, 2 * d, w_low)
                cum = _fox_cum(f_logit, _pad_last(kv_b_f[None, :], LANE))
                cum = cum[:, :fox_heads, None, :]
            q = _nm_matmul(x, norm1_g[layer], sh1, sc1, fox_w_q_b, j, d,
                           out_scale=float(d // fox_heads) ** -0.5 * LOG2_E)
            mix = _fox_attn(q, kv, cum, heads=fox_heads)
            x = _matmul_res(mix, fox_w_o_b, j, x, gt1)
        x = _ffn(x, norm2_g[layer], sh2, sc2, gt2, w_up_b, conv_w_p, conv_b_p, w_down_b, layer,
                 final_g if layer == depth - 1 else None, tf=FFN_TF)
    return x
```

```python
import functools
import math

import jax
import jax.numpy as jnp
from jax import lax
from jax.experimental import pallas as pl
from jax.experimental.pallas import tpu as pltpu

BF16 = jnp.bfloat16
F32 = jnp.float32

NORM_EPS = 1e-6
GLA_CHUNK = 64
GLA_BLOCK = 256
GLA_GATE_TAU = 16.0
LOG2_E = math.log2(math.e)

LANE = 128
SUBLANE = 8
VMEM_LIMIT_BYTES = 56 * 1024 * 1024
NORM_ROWS = 2 * SUBLANE
NORM_UNROLL = 4

_NT = (((1,), (1,)), ((), ()))
_TN = (((0,), (0,)), ((), ()))


def _params(*sem):
    return pltpu.CompilerParams(dimension_semantics=sem, vmem_limit_bytes=VMEM_LIMIT_BYTES)


def _pick(n, pref):
    t = min(n, pref)
    while n % t:
        t //= 2
    return t


def _round_up(n, m):
    return (n + m - 1) // m * m


def _silu(x):
    return x * jax.nn.sigmoid(x)


def _log_sigmoid(x):
    return jnp.minimum(x, 0.0) - jnp.log1p(jnp.exp(-jnp.abs(x)))


def _split_bf16(x, pieces):
    out = []
    rem = x
    for _ in range(pieces):
        p = rem.astype(BF16)
        out.append(p)
        rem = rem - p.astype(F32)
    return out


def _dot(a, b):
    return jnp.dot(a, b, preferred_element_type=F32)


def _dot_exact_lhs(a_bf16, x, pieces):
    acc = None
    for p in _split_bf16(x, pieces):
        t = _dot(a_bf16, p)
        acc = t if acc is None else acc + t
    return acc


def _dot3(a, b):
    a1, a2 = _split_bf16(a, 2)
    b1, b2 = _split_bf16(b, 2)
    return _dot(a1, b1) + (_dot(a1, b2) + _dot(a2, b1))


def _rms_scale(x):
    return lax.rsqrt(jnp.mean(x * x, axis=-1, keepdims=True) + NORM_EPS)


def _norm_mod_to(h_ref, x_ref, g_ref, sh_ref, sc_ref):
    rows = x_ref.shape[1]
    step = _pick(rows, NORM_ROWS)
    g = g_ref[...]
    shift = sh_ref[0]
    scale1 = 1.0 + sc_ref[0]

    def body(r, carry):
        sl = pl.ds(pl.multiple_of(r * step, step), step)
        inv = _rms_scale(x_ref[0, sl, :])
        y = (x_ref[0, sl, :] * inv) * g
        h_ref[sl, :] = (y * scale1 + shift).astype(h_ref.dtype)
        return carry

    lax.fori_loop(0, rows // step, body, 0, unroll=NORM_UNROLL)


def _ada_kernel(c_ref, w_ref, b_ref, o_ref):
    ca = _silu(c_ref[...]).astype(BF16)
    o_ref[0] = _dot(ca, w_ref[0].astype(BF16)) + b_ref[0]


def _ada(c, w, b):
    n_layers, d, n = w.shape
    bsz = c.shape[0]
    tn = _pick(n, 512)
    return pl.pallas_call(
        _ada_kernel,
        grid=(n_layers, n // tn),
        in_specs=[
            pl.BlockSpec((bsz, d), lambda l, j: (0, 0)),
            pl.BlockSpec((1, d, tn), lambda l, j: (l, 0, j)),
            pl.BlockSpec((1, 1, tn), lambda l, j: (l, 0, j)),
        ],
        out_specs=pl.BlockSpec((1, bsz, tn), lambda l, j: (l, 0, j)),
        out_shape=jax.ShapeDtypeStruct((n_layers, bsz, n), F32),
        compiler_params=_params("arbitrary", "arbitrary"),
        name="ada_mod",
    )(c, w, b.reshape(n_layers, 1, n))


def _nm_matmul_kernel(*refs, out_scale, has_small):
    if has_small:
        x_ref, g_ref, sh_ref, sc_ref, w_ref, ws_ref, o_ref, os_ref, h_ref = refs
    else:
        x_ref, g_ref, sh_ref, sc_ref, w_ref, o_ref, h_ref = refs

    @pl.when(pl.program_id(2) == 0)
    def _():
        _norm_mod_to(h_ref, x_ref, g_ref, sh_ref, sc_ref)
        if has_small:
            os_ref[0] = _dot(h_ref[...], ws_ref[...].astype(BF16))

    acc = _dot(h_ref[...], w_ref[...])
    if out_scale != 1.0:
        acc = acc * out_scale
    o_ref[0] = acc.astype(o_ref.dtype)


def _nm_matmul(x, g, shift, scale, w, layer, w_small=None, *, out_scale=1.0, tm=1024):
    bsz, s, d = x.shape
    tm = _pick(s, tm)
    tn = w.shape[3]
    n_out = w.shape[1] * tn
    has_small = w_small is not None
    in_specs = [
        pl.BlockSpec((1, tm, d), lambda b, i, j: (b, i, 0), pipeline_mode=pl.Buffered(1)),
        pl.BlockSpec((1, d), lambda b, i, j: (0, 0)),
        pl.BlockSpec((1, 1, d), lambda b, i, j: (b, 0, 0)),
        pl.BlockSpec((1, 1, d), lambda b, i, j: (b, 0, 0)),
        pl.BlockSpec((None, None, d, tn), lambda b, i, j: (layer, j, 0, 0)),
    ]
    args = [x, g.reshape(1, d), shift, scale, w]
    out_specs = [pl.BlockSpec((1, tm, tn), lambda b, i, j: (b, i, j))]
    out_shape = [jax.ShapeDtypeStruct((bsz, s, n_out), BF16)]
    if has_small:
        ns = w_small.shape[1]
        in_specs.append(pl.BlockSpec((d, ns), lambda b, i, j: (0, 0)))
        args.append(w_small)
        out_specs.append(pl.BlockSpec((1, tm, ns), lambda b, i, j: (b, i, 0)))
        out_shape.append(jax.ShapeDtypeStruct((bsz, s, ns), F32))
    res = pl.pallas_call(
        functools.partial(_nm_matmul_kernel, out_scale=out_scale, has_small=has_small),
        grid=(bsz, s // tm, n_out // tn),
        in_specs=in_specs,
        out_specs=out_specs,
        out_shape=out_shape,
        scratch_shapes=[pltpu.VMEM((tm, d), BF16)],
        compiler_params=_params("arbitrary", "arbitrary", "arbitrary"),
        name="norm_mod_proj",
    )(*args)
    return res if has_small else res[0]


def _matmul_res_kernel(a_ref, w_ref, x_ref, gt_ref, o_ref):
    o_ref[0] = x_ref[0] + gt_ref[0] * _dot(a_ref[0], w_ref[...])


def _matmul_res(a, w, layer, x, gate, *, tm=1024):
    bsz, s, k = a.shape
    tn = w.shape[3]
    n = w.shape[1] * tn
    tm = _pick(s, tm)
    return pl.pallas_call(
        _matmul_res_kernel,
        grid=(bsz, s // tm, n // tn),
        in_specs=[
            pl.BlockSpec((1, tm, k), lambda b, i, j: (b, i, 0)),
            pl.BlockSpec((None, None, k, tn), lambda b, i, j: (layer, j, 0, 0)),
            pl.BlockSpec((1, tm, tn), lambda b, i, j: (b, i, j)),
            pl.BlockSpec((1, 1, tn), lambda b, i, j: (b, 0, j)),
        ],
        out_specs=pl.BlockSpec((1, tm, tn), lambda b, i, j: (b, i, j)),
        out_shape=jax.ShapeDtypeStruct((bsz, s, n), F32),
        compiler_params=_params("arbitrary", "arbitrary", "arbitrary"),
        name="proj_residual",
    )(a, w, x, gate)


def _gla_kernel(q_ref, k_ref, v_ref, r_ref, gl_ref, wgu_ref, bg_ref, br_ref, hg_ref,
                o_ref, qd_ref, kd_ref, oi_ref, dec_ref, st_ref, *, chunk, blk, group):
    s, hk = qd_ref.shape
    per_blk = blk // chunk
    q_scale = float(hk) ** -0.5

    row = lax.broadcasted_iota(jnp.int32, (blk, blk), 0)
    col = lax.broadcasted_iota(jnp.int32, (blk, blk), 1)
    causal = (row - col).astype(jnp.uint32) <= (row & (chunk - 1)).astype(jnp.uint32)
    tri = jnp.where(causal, 1.0, 0.0).astype(BF16)

    def block_body(i, carry):
        ids = [i * group + j for j in range(group)]
        rows = [pl.ds(pl.multiple_of(n * blk, blk), blk) for n in ids]
        z = [_dot3(gl_ref[0, r, :], wgu_ref[...]) + bg_ref[...] for r in rows]
        log_a = [_log_sigmoid(t) / GLA_GATE_TAU for t in z]
        b = [_dot_exact_lhs(tri, t, 3) for t in log_a]
        last = [[t[(c + 1) * chunk - 1:(c + 1) * chunk, :] for c in range(per_blk)] for t in b]
        q_dec = [(q_ref[0, r, :].astype(F32) * q_scale * jnp.exp(t)).astype(BF16)
                 for r, t in zip(rows, b)]
        k_inv = [(k_ref[0, r, :].astype(F32) * jnp.exp(-t)).astype(BF16) for r, t in zip(rows, b)]
        scores = [lax.dot_general(qd, ki, _NT, preferred_element_type=F32)
                  for qd, ki in zip(q_dec, k_inv)]
        for r, qd, t, ends in zip(rows, q_dec, b, last):
            qd_ref[r, :] = qd
            b_last = jnp.concatenate([jnp.broadcast_to(e, (chunk, hk)) for e in ends], axis=0)
            kd_ref[r, :] = (k_ref[0, r, :].astype(F32) * jnp.exp(b_last - t)).astype(BF16)
        scores = [jnp.where(causal, t, 0.0).astype(BF16) for t in scores]
        for r, sc in zip(rows, scores):
            oi_ref[r, :] = _dot(sc, v_ref[0, r, :])
        for n, ends in zip(ids, last):
            for c, e in enumerate(ends):
                dec_rows = pl.ds(pl.multiple_of((n * per_blk + c) * SUBLANE, SUBLANE), SUBLANE)
                dec_ref[dec_rows, :] = jnp.broadcast_to(jnp.exp(e), (SUBLANE, hk))
        return carry

    lax.fori_loop(0, s // (blk * group), block_body, 0)

    st_ref[...] = jnp.zeros_like(st_ref)

    def chunk_body(c, carry):
        rows = pl.ds(pl.multiple_of(c * chunk, chunk), chunk)
        st = st_ref[...]
        o = oi_ref[rows, :] + lax.dot_general(qd_ref[rows, :], st.astype(BF16), _NT,
                                              preferred_element_type=F32)
        decay = dec_ref[pl.ds(pl.multiple_of(c * SUBLANE, SUBLANE), SUBLANE), :][0:1, :]
        st_ref[...] = st * decay + lax.dot_general(
            v_ref[0, rows, :], kd_ref[rows, :], _TN, preferred_element_type=F32)
        o = (o * _rms_scale(o)) * hg_ref[...]
        gate = _silu(r_ref[0, rows, :].astype(F32) + br_ref[...])
        o_ref[0, rows, :] = (o * gate).astype(o_ref.dtype)
        return carry

    lax.fori_loop(0, s // chunk, chunk_body, 0, unroll=8)


def _gla(proj, g_low, w_gate_up, b_gate, b_r, head_g, *, heads, dk, dv):
    bsz, s, _ = proj.shape
    hk, hv = dk // heads, dv // heads
    nl = g_low.shape[-1]
    k_blk = dk // hk
    v_blk = (2 * dk) // hv
    r_blk = (2 * dk + dv) // hv
    blk = _pick(s, GLA_BLOCK)
    assert (2 * dk) % hv == 0 and blk % GLA_CHUNK == 0 and GLA_CHUNK & (GLA_CHUNK - 1) == 0
    return pl.pallas_call(
        functools.partial(_gla_kernel, chunk=GLA_CHUNK, blk=blk,
                          group=2 if (s // blk) % 2 == 0 else 1),
        grid=(bsz, heads),
        in_specs=[
            pl.BlockSpec((1, s, hk), lambda b, h: (b, 0, h)),
            pl.BlockSpec((1, s, hk), lambda b, h: (b, 0, k_blk + h)),
            pl.BlockSpec((1, s, hv), lambda b, h: (b, 0, v_blk + h)),
            pl.BlockSpec((1, s, hv), lambda b, h: (b, 0, r_blk + h)),
            pl.BlockSpec((1, s, nl), lambda b, h: (b, 0, 0)),
            pl.BlockSpec((nl, hk), lambda b, h: (0, h)),
            pl.BlockSpec((1, hk), lambda b, h: (0, h)),
            pl.BlockSpec((1, hv), lambda b, h: (0, h)),
            pl.BlockSpec((1, hv), lambda b, h: (0, 0)),
        ],
        out_specs=pl.BlockSpec((1, s, hv), lambda b, h: (b, 0, h)),
        out_shape=jax.ShapeDtypeStruct((bsz, s, dv), BF16),
        scratch_shapes=[
            pltpu.VMEM((s, hk), BF16),
            pltpu.VMEM((s, hk), BF16),
            pltpu.VMEM((s, hv), F32),
            pltpu.VMEM((s // GLA_CHUNK * SUBLANE, hk), F32),
            pltpu.VMEM((hv, hk), F32),
        ],
        compiler_params=_params("arbitrary", "arbitrary"),
        name="gla_mixer",
    )(proj, proj, proj, proj, g_low, w_gate_up, b_gate.reshape(1, dk), b_r.reshape(1, dv),
      head_g.reshape(1, hv))


def _fox_cum_kernel(f_ref, bf_ref, o_ref, cum_ref, *, blk):
    s = f_ref.shape[1]
    row = lax.broadcasted_iota(jnp.int32, (blk, blk), 0)
    col = lax.broadcasted_iota(jnp.int32, (blk, blk), 1)
    tri = jnp.where(row >= col, 1.0, 0.0).astype(BF16)
    carry = jnp.zeros((1, f_ref.shape[2]), F32)
    for i in range(s // blk):
        rows = slice(i * blk, (i + 1) * blk)
        lf = _log_sigmoid(f_ref[0, rows, :] + bf_ref[...])
        cum = _dot_exact_lhs(tri, lf, 3) + carry
        cum_ref[rows, :] = cum * LOG2_E
        carry = cum[blk - 1:blk, :]
    o_ref[0] = cum_ref[...].T


def _fox_cum(f_logit, b_f):
    bsz, s, nl = f_logit.shape
    blk = _pick(s, 256)
    return pl.pallas_call(
        functools.partial(_fox_cum_kernel, blk=blk),
        grid=(bsz,),
        in_specs=[
            pl.BlockSpec((1, s, nl), lambda b: (b, 0, 0)),
            pl.BlockSpec((1, nl), lambda b: (0, 0)),
        ],
        out_specs=pl.BlockSpec((1, nl, s), lambda b: (b, 0, 0)),
        out_shape=jax.ShapeDtypeStruct((bsz, nl, s), F32),
        scratch_shapes=[pltpu.VMEM((s, nl), F32)],
        compiler_params=_params("arbitrary"),
        name="fox_cum",
    )(f_logit, b_f)


def _lane_fold(x, op):
    out = x[:, :LANE]
    for c in range(1, x.shape[1] // LANE):
        out = op(out, x[:, c * LANE:(c + 1) * LANE])
    return out


def _fox_attn_kernel(q_ref, k_ref, v_ref, cum_ref, o_ref, t_ref, *, tq, tk):
    s, hd = q_ref.shape[1], q_ref.shape[2]
    eye = (lax.broadcasted_iota(jnp.int32, (tq, tq), 0)
           == lax.broadcasted_iota(jnp.int32, (tq, tq), 1))
    for i in range(s // tq):
        q0, q1 = i * tq, (i + 1) * tq
        q = q_ref[0, q0:q1, :]
        cum_q = jnp.sum(jnp.where(eye, cum_ref[0, 0, :, q0:q1], 0.0), axis=1, keepdims=True)
        chunks = [(k0, min(tk, q1 - k0)) for k0 in range(0, q1, tk)]
        mx = None
        for k0, w in chunks:
            t = lax.dot_general(q, k_ref[0, k0:k0 + w, :], _NT, preferred_element_type=F32)
            t = t - cum_ref[0, 0, :, k0:k0 + w]
            if k0 + w - 1 > q0:
                qpos = q0 + lax.broadcasted_iota(jnp.int32, (tq, w), 0)
                kpos = k0 + lax.broadcasted_iota(jnp.int32, (tq, w), 1)
                t = jnp.where(qpos >= kpos, t, -jnp.inf)
            t_ref[:, k0:k0 + w] = t
            cm = _lane_fold(t, jnp.maximum)
            mx = cm if mx is None else jnp.maximum(mx, cm)
        m = jnp.max(mx, axis=1, keepdims=True) + cum_q
        shift = cum_q - m
        den = jnp.zeros((tq, LANE), F32)
        acc = jnp.zeros((tq, hd), F32)
        for k0, w in chunks:
            p = jnp.exp2(t_ref[:, k0:k0 + w] + shift)
            den = den + _lane_fold(p, jnp.add)
            acc = acc + _dot(p.astype(BF16), v_ref[0, k0:k0 + w, :])
        o_ref[0, q0:q1, :] = (acc / jnp.sum(den, axis=1, keepdims=True)).astype(o_ref.dtype)


def _fox_attn(q, kv, cum, *, heads, tq=256, tk=256):
    bsz, s, d = q.shape
    hd = d // heads
    tq = _pick(s, tq)
    tk = _pick(s, tk)
    return pl.pallas_call(
        functools.partial(_fox_attn_kernel, tq=tq, tk=tk),
        grid=(bsz, heads),
        in_specs=[
            pl.BlockSpec((1, s, hd), lambda b, h: (b, 0, h)),
            pl.BlockSpec((1, s, hd), lambda b, h: (b, 0, h)),
            pl.BlockSpec((1, s, hd), lambda b, h: (b, 0, heads + h)),
            pl.BlockSpec((1, 1, 1, s), lambda b, h: (b, h, 0, 0)),
        ],
        out_specs=pl.BlockSpec((1, s, hd), lambda b, h: (b, 0, h)),
        out_shape=jax.ShapeDtypeStruct((bsz, s, d), BF16),
        scratch_shapes=[pltpu.VMEM((tq, s), F32)],
        compiler_params=_params("arbitrary", "arbitrary"),
        name="fox_attention",
    )(q, kv, kv, cum)


def _ffn_kernel(*refs, final_norm, n_sub):
    if final_norm:
        (x_ref, g_ref, sh_ref, sc_ref, gt_ref, wa_ref, wg_ref, cwa_ref, cwg_ref, cba_ref,
         cbg_ref, wd_ref, fg_ref, o_ref, h_ref, ua_ref, ug_ref, ta_ref, tg_ref) = refs
    else:
        (x_ref, g_ref, sh_ref, sc_ref, gt_ref, wa_ref, wg_ref, cwa_ref, cwg_ref, cba_ref,
         cbg_ref, wd_ref, o_ref, h_ref, ua_ref, ug_ref, ta_ref, tg_ref) = refs
    i = pl.program_id(1)
    f = pl.program_id(2)
    nf = pl.num_programs(2)
    tm = x_ref.shape[1]
    halo = SUBLANE

    @pl.when(f == 0)
    def _():
        _norm_mod_to(h_ref, x_ref, g_ref, sh_ref, sc_ref)
        o_ref[...] = jnp.zeros_like(o_ref)

    @pl.when(i == 0)
    def _():
        ua_ref[0:halo, :] = jnp.zeros((halo, ua_ref.shape[1]), F32)
        ug_ref[0:halo, :] = jnp.zeros((halo, ug_ref.shape[1]), F32)

    @pl.when(i > 0)
    def _():
        ua_ref[0:halo, :] = ta_ref[f]
        ug_ref[0:halo, :] = tg_ref[f]

    def conv(u_ref, cw_ref, cb_ref, r0):
        acc = cw_ref[0:1, :] * u_ref[r0 + halo - 2:r0 + halo - 2 + sub, :]
        acc = acc + cw_ref[1:2, :] * u_ref[r0 + halo - 1:r0 + halo - 1 + sub, :]
        acc = acc + cw_ref[2:3, :] * u_ref[r0 + halo:r0 + halo + sub, :]
        return acc + cb_ref[...]

    sub = tm // n_sub
    for r in range(n_sub):
        r0 = r * sub
        h = h_ref[r0:r0 + sub, :]
        ua = _dot(h, wa_ref[...])
        ug = _dot(h, wg_ref[...])
        ua_ref[halo + r0:halo + r0 + sub, :] = ua
        ug_ref[halo + r0:halo + r0 + sub, :] = ug
        if r == n_sub - 1:
            ta_ref[f] = ua[sub - halo:sub, :]
            tg_ref[f] = ug[sub - halo:sub, :]
    for r in range(n_sub):
        r0 = r * sub
        a = conv(ua_ref, cwa_ref, cba_ref, r0)
        g = conv(ug_ref, cwg_ref, cbg_ref, r0)
        o_ref[0, r0:r0 + sub, :] += _dot((_silu(a) * g).astype(BF16), wd_ref[...])

    @pl.when(f == nf - 1)
    def _():
        step = _pick(tm, NORM_ROWS)

        def body(r, carry):
            sl = pl.ds(pl.multiple_of(r * step, step), step)
            y = x_ref[0, sl, :] + gt_ref[0] * o_ref[0, sl, :]
            if final_norm:
                y = (y * _rms_scale(y)) * fg_ref[...]
            o_ref[0, sl, :] = y
            return carry

        lax.fori_loop(0, tm // step, body, 0, unroll=NORM_UNROLL)


def _ffn(x, g, shift, scale, gate, w_up, conv_w, conv_b, w_down, layer, final_g=None, *,
         tm=512, tf=512, n_sub=4):
    bsz, s, d = x.shape
    ff = w_down.shape[1]
    tm = _pick(s, tm)
    assert ff % tf == 0 and conv_w.shape[1] == 3
    nf = ff // tf
    final_norm = final_g is not None
    once = pl.Buffered(1)
    in_specs = [
        pl.BlockSpec((1, tm, d), lambda b, i, f: (b, i, 0), pipeline_mode=once),
        pl.BlockSpec((1, d), lambda b, i, f: (0, 0)),
        pl.BlockSpec((1, 1, d), lambda b, i, f: (b, 0, 0)),
        pl.BlockSpec((1, 1, d), lambda b, i, f: (b, 0, 0)),
        pl.BlockSpec((1, 1, d), lambda b, i, f: (b, 0, 0)),
        pl.BlockSpec((None, None, None, d, tf), lambda b, i, f: (layer, 0, f, 0, 0)),
        pl.BlockSpec((None, None, None, d, tf), lambda b, i, f: (layer, 1, f, 0, 0)),
        pl.BlockSpec((None, 3, tf), lambda b, i, f: (layer, 0, f)),
        pl.BlockSpec((None, 3, tf), lambda b, i, f: (layer, 0, nf + f)),
        pl.BlockSpec((None, 1, tf), lambda b, i, f: (layer, 0, f)),
        pl.BlockSpec((None, 1, tf), lambda b, i, f: (layer, 0, nf + f)),
        pl.BlockSpec((None, tf, d), lambda b, i, f: (layer, f, 0)),
    ]
    args = [x, g.reshape(1, d), shift, scale, gate, w_up, w_up, conv_w, conv_w, conv_b, conv_b,
            w_down]
    if final_norm:
        in_specs.append(pl.BlockSpec((1, d), lambda b, i, f: (0, 0)))
        args.append(final_g.reshape(1, d))
    return pl.pallas_call(
        functools.partial(_ffn_kernel, final_norm=final_norm, n_sub=n_sub),
        grid=(bsz, s // tm, nf),
        in_specs=in_specs,
        out_specs=pl.BlockSpec((1, tm, d), lambda b, i, f: (b, i, 0), pipeline_mode=once),
        out_shape=jax.ShapeDtypeStruct((bsz, s, d), F32),
        scratch_shapes=[
            pltpu.VMEM((tm, d), BF16),
            pltpu.VMEM((tm + SUBLANE, tf), F32),
            pltpu.VMEM((tm + SUBLANE, tf), F32),
            pltpu.VMEM((nf, SUBLANE, tf), F32),
            pltpu.VMEM((nf, SUBLANE, tf), F32),
        ],
        compiler_params=_params("arbitrary", "arbitrary", "arbitrary"),
        name="conv_ffn",
    )(*args)


FFN_TF = 512


def _prep_kernel(lo_ref, hi_ref, o_ref, *, axis, n_src, tile_axis):
    half = lo_ref.shape[axis]
    hi_valid = 2 * pl.program_id(tile_axis) + 1 < n_src
    lo_sl = (slice(0, half), slice(None)) if axis == 0 else (slice(None), slice(0, half))
    hi_sl = (slice(half, 2 * half), slice(None)) if axis == 0 else (slice(None), slice(half, 2 * half))
    o_ref[lo_sl] = lo_ref[...].astype(o_ref.dtype)

    @pl.when(hi_valid)
    def _():
        o_ref[hi_sl] = hi_ref[...].astype(o_ref.dtype)

    @pl.when(jnp.logical_not(hi_valid))
    def _():
        o_ref[hi_sl] = jnp.zeros(hi_ref.shape, o_ref.dtype)


def _cast_kernel(x_ref, o_ref):
    o_ref[...] = x_ref[...].astype(o_ref.dtype)


def _prep_tiles(w, n_out, tn=1024, k_split=2):
    n_layers, k, _ = w.shape
    tn = _pick(n_out, tn)
    tk = k // k_split
    return pl.pallas_call(
        _cast_kernel,
        grid=(n_layers, n_out // tn, k_split),
        in_specs=[pl.BlockSpec((None, tk, tn), lambda l, j, kk: (l, kk, j))],
        out_specs=pl.BlockSpec((None, None, tk, tn), lambda l, j, kk: (l, j, kk, 0)),
        out_shape=jax.ShapeDtypeStruct((n_layers, n_out // tn, k, tn), BF16),
        compiler_params=_params("arbitrary", "arbitrary", "arbitrary"),
        name="prep_tiles",
    )(w)


def _prep_w_up(w, ff, tf):
    n_layers, d, _ = w.shape
    half = tf // 2
    assert ff % half == 0
    n_src = ff // half
    n_tiles = (n_src + 1) // 2
    return pl.pallas_call(
        functools.partial(_prep_kernel, axis=1, n_src=n_src, tile_axis=2),
        grid=(n_layers, 2, n_tiles),
        in_specs=[
            pl.BlockSpec((None, d, half), lambda l, p, f: (l, 0, p * n_src + 2 * f)),
            pl.BlockSpec((None, d, half),
                         lambda l, p, f: (l, 0, p * n_src + jnp.minimum(2 * f + 1, n_src - 1))),
        ],
        out_specs=pl.BlockSpec((None, None, None, d, tf), lambda l, p, f: (l, p, f, 0, 0)),
        out_shape=jax.ShapeDtypeStruct((n_layers, 2, n_tiles, d, tf), BF16),
        compiler_params=_params("arbitrary", "arbitrary", "arbitrary"),
        name="prep_w_up",
    )(w, w)


def _prep_w_down(w, tf):
    n_layers, ff, d = w.shape
    half = tf // 2
    assert ff % half == 0
    n_src = ff // half
    n_tiles = (n_src + 1) // 2
    return pl.pallas_call(
        functools.partial(_prep_kernel, axis=0, n_src=n_src, tile_axis=1),
        grid=(n_layers, n_tiles),
        in_specs=[
            pl.BlockSpec((None, half, d), lambda l, f: (l, 2 * f, 0)),
            pl.BlockSpec((None, half, d), lambda l, f: (l, jnp.minimum(2 * f + 1, n_src - 1), 0)),
        ],
        out_specs=pl.BlockSpec((None, tf, d), lambda l, f: (l, f, 0)),
        out_shape=jax.ShapeDtypeStruct((n_layers, n_tiles * tf, d), BF16),
        compiler_params=_params("arbitrary", "arbitrary"),
        name="prep_w_down",
    )(w, w)


def _pad_last(w, n):
    return jnp.pad(w, [(0, 0)] * (w.ndim - 1) + [(0, n - w.shape[-1])])


def _pad_halves(w, ff, ffp):
    return jnp.concatenate([_pad_last(w[..., :ff], ffp), _pad_last(w[..., ff:], ffp)], axis=-1)


def kernel(x, c, ada_w, ada_b, norm1_g, norm2_g, gla_w_in, gla_w_gate_up, gla_b_gate, gla_b_r,
           gla_head_g, gla_w_o, kv_ada_w, kv_ada_b, kv_norm_g, kv_w, kv_b_f, fox_w_q, fox_w_o,
           ffn_w_up, ffn_conv_w, ffn_conv_b, ffn_w_down, final_g):
    bsz, s, d = x.shape
    depth = ada_w.shape[0]
    n_a = gla_w_in.shape[0]
    dk = gla_w_gate_up.shape[2]
    dv = gla_b_r.shape[1]
    gla_heads = dv // gla_head_g.shape[1]
    gate_rank = gla_w_gate_up.shape[1]
    fox_heads = kv_b_f.shape[0]
    ff = ffn_w_down.shape[1]
    ffp = _round_up(ff, FFN_TF)

    n_main = 2 * dk + 2 * dv
    gla_w_in_b = _prep_tiles(gla_w_in, n_main)
    gla_w_o_b = _prep_tiles(gla_w_o, d)
    kv_w_b = _prep_tiles(kv_w[None], 2 * d)
    fox_w_q_b = _prep_tiles(fox_w_q, d)
    fox_w_o_b = _prep_tiles(fox_w_o, d)
    w_up_b = _prep_w_up(ffn_w_up, ff, FFN_TF)
    w_down_b = _prep_w_down(ffn_w_down, FFN_TF)
    conv_w_p = _pad_halves(ffn_conv_w, ff, ffp)
    conv_b_p = _pad_halves(ffn_conv_b, ff, ffp)[:, None, :]

    mods = _ada(c, ada_w, ada_b)
    kv_mod = _ada(c, kv_ada_w[None], kv_ada_b[None])[0]

    def chunks(m, n):
        return [m[:, None, k * d:(k + 1) * d] for k in range(n)]

    for layer in range(depth):
        sh1, sc1, gt1, sh2, sc2, gt2 = chunks(mods[layer], 6)
        if layer < n_a:
            w_low = _pad_last(gla_w_in[layer][:, n_main:], LANE)
            proj, g_low = _nm_matmul(x, norm1_g[layer], sh1, sc1, gla_w_in_b, layer, w_low)
            w_gu = jnp.pad(gla_w_gate_up[layer], ((0, LANE - gate_rank), (0, 0)))
            mix = _gla(proj, g_low, w_gu, gla_b_gate[layer], gla_b_r[layer], gla_head_g[layer],
                       heads=gla_heads, dk=dk, dv=dv)
            x = _matmul_res(mix, gla_w_o_b, layer, x, gt1)
        else:
            j = layer - n_a
            if layer == n_a:
                kv_sh, kv_sc = chunks(kv_mod, 2)
                w_low = _pad_last(kv_w[:, 2 * d:], LANE)
                kv, f_logit = _nm_matmul(x, kv_norm_g, kv_sh, kv_sc, kv_w_b, 0, w_low)
                cum = _fox_cum(f_logit, _pad_last(kv_b_f[None, :], LANE))
                cum = cum[:, :fox_heads, None, :]
            q = _nm_matmul(x, norm1_g[layer], sh1, sc1, fox_w_q_b, j,
                           out_scale=float(d // fox_heads) ** -0.5 * LOG2_E)
            mix = _fox_attn(q, kv, cum, heads=fox_heads)
            x = _matmul_res(mix, fox_w_o_b, j, x, gt1)
        x = _ffn(x, norm2_g[layer], sh2, sc2, gt2, w_up_b, conv_w_p, conv_b_p, w_down_b, layer,
                 final_g if layer == depth - 1 else None, tf=FFN_TF)
    return x
```
